```python
import math
import jax, jax.numpy as jnp
from jax import lax
import numpy as np

D_MODEL = 1024
BATCH = 8
SEQ = 4096
DEPTH = 1

N_META = 16
BLOCK = 128
PREFIX = BLOCK
N_PAD = PREFIX - N_META
ATT_HEADS = 16
ATT_KV_HEADS = 2
ATT_GROUP = ATT_HEADS // ATT_KV_HEADS
ATT_HEAD_DIM = 64
WINDOW = 128
RET_HEADS = 4
RET_KEY_DIM = 128
RET_VALUE_DIM = 256
ATT_Q_W = ATT_HEADS * ATT_HEAD_DIM
ATT_KV_W = ATT_KV_HEADS * ATT_HEAD_DIM
RET_QK_W = RET_HEADS * RET_KEY_DIM
RET_V_W = RET_HEADS * RET_VALUE_DIM
GATE_W = 2 * D_MODEL
SPLIT_POINTS = (ATT_Q_W,
                ATT_Q_W + ATT_KV_W,
                ATT_Q_W + 2 * ATT_KV_W,
                ATT_Q_W + 2 * ATT_KV_W + RET_QK_W,
                ATT_Q_W + 2 * ATT_KV_W + 2 * RET_QK_W,
                ATT_Q_W + 2 * ATT_KV_W + 2 * RET_QK_W + RET_V_W,
                ATT_Q_W + 2 * ATT_KV_W + 2 * RET_QK_W + 2 * RET_V_W)
IN_WIDTH = ATT_Q_W + 2 * ATT_KV_W + 2 * RET_QK_W + 2 * RET_V_W + GATE_W
N_EXPERTS = 32
TOP_K = 4
D_FF = 1024
SWIGLU_ALPHA = 1.702
SWIGLU_LIMIT = 7.0
MOE_BLOCK = 256
EPS = 1e-5

kernel_name = "hybrid_swa_retention_moe_meta"


def rms_norm(x, gain=None):
    xf = x.astype(jnp.float32)
    y = xf * lax.rsqrt(jnp.mean(xf * xf, axis=-1, keepdims=True) + EPS)
    if gain is not None:
        y = y * gain.astype(jnp.float32)
    return y.astype(x.dtype)


def alibi_slopes(n_heads):
    return 2.0 ** (-8.0 * jnp.arange(1, n_heads + 1, dtype=jnp.float32) / n_heads)


def sliding_window_attention(q, k, v, sinks):
    b, l = q.shape[:2]
    nb = l // BLOCK
    qb = q.reshape(b, nb, BLOCK, ATT_KV_HEADS, ATT_GROUP, ATT_HEAD_DIM)
    kb = k.reshape(b, nb, BLOCK, ATT_KV_HEADS, ATT_HEAD_DIM)
    vb = v.reshape(b, nb, BLOCK, ATT_KV_HEADS, ATT_HEAD_DIM)
    pad_prev = ((0, 0), (1, 0), (0, 0), (0, 0), (0, 0))
    k_band = jnp.concatenate([jnp.pad(kb[:, :-1], pad_prev), kb], axis=2)
    v_band = jnp.concatenate([jnp.pad(vb[:, :-1], pad_prev), vb], axis=2)
    k_meta = k[:, N_PAD:PREFIX]
    v_meta = v[:, N_PAD:PREFIX]
    meta_pos = jnp.arange(N_PAD, PREFIX)
    q_pos = jnp.arange(l).reshape(nb, BLOCK)
    slopes = alibi_slopes(ATT_HEADS).reshape(ATT_KV_HEADS, ATT_GROUP)
    sink = sinks.astype(jnp.float32).reshape(ATT_KV_HEADS, ATT_GROUP)
    scale = ATT_HEAD_DIM ** -0.5

    def one_block(args):
        q_blk, kb_blk, vb_blk, qp = args
        kp = jnp.concatenate([qp - BLOCK, qp])
        d_band = qp[:, None] - kp[None, :]
        ok_band = (d_band >= 0) & (d_band < WINDOW) & (kp[None, :] >= PREFIX)
        d_meta = qp[:, None] - meta_pos[None, :]
        ok_meta = d_meta >= 0
        dist = jnp.abs(jnp.concatenate([d_meta, d_band], axis=1)).astype(jnp.float32)
        ok = jnp.concatenate([ok_meta, ok_band], axis=1)
        keys = jnp.concatenate([k_meta, kb_blk], axis=1)
        vals = jnp.concatenate([v_meta, vb_blk], axis=1)
        s = jnp.einsum('bqhgd,bkhd->bhgqk', q_blk, keys).astype(jnp.float32) * scale
        s = s - slopes[:, :, None, None] * dist
        s = jnp.where(ok, s, -jnp.inf)
        sink_col = jnp.broadcast_to(sink[None, :, :, None, None], s.shape[:-1] + (1,))
        p = jax.nn.softmax(jnp.concatenate([s, sink_col], axis=-1), axis=-1)[..., :-1]
        return jnp.einsum('bhgqk,bkhd->bqhgd', p.astype(vals.dtype), vals)

    out = lax.map(one_block, (jnp.moveaxis(qb, 1, 0), jnp.moveaxis(k_band, 1, 0),
                              jnp.moveaxis(v_band, 1, 0), q_pos))
    return jnp.moveaxis(out, 0, 1).reshape(b, l, ATT_Q_W)


def chunkwise_retention(q, k, v):
    out_dtype = q.dtype
    q, k, v = (a.astype(jnp.float32) for a in (q, k, v))
    b, l = q.shape[:2]
    nc = l // BLOCK
    log_g = jnp.log1p(-(2.0 ** (-5.0 - jnp.arange(RET_HEADS, dtype=jnp.float32))))
    idx = jnp.arange(BLOCK, dtype=jnp.float32)
    diff = idx[:, None] - idx[None, :]
    decay_intra = jnp.where(diff >= 0, jnp.exp(log_g[:, None, None] * jnp.maximum(diff, 0.0)), 0.0)
    decay_in = jnp.exp(log_g[None, :] * (idx[:, None] + 1.0))
    decay_out = jnp.exp(log_g[None, :] * (BLOCK - 1.0 - idx[:, None]))
    decay_chunk = jnp.exp(log_g * BLOCK)
    qc = q.reshape(b, nc, BLOCK, RET_HEADS, RET_KEY_DIM)
    kc = k.reshape(b, nc, BLOCK, RET_HEADS, RET_KEY_DIM)
    vc = v.reshape(b, nc, BLOCK, RET_HEADS, RET_VALUE_DIM)
    s = jnp.einsum('bnqhd,bnkhd->bnhqk', qc, kc) * decay_intra
    intra = jnp.einsum('bnhqk,bnkhe->bnqhe', s, vc)
    kv = jnp.einsum('bnkhd,bnkhe->bnhde', kc * decay_out[None, None, :, :, None], vc)

    def step(state, kv_n):
        return state * decay_chunk[:, None, None] + kv_n, state

    init = jnp.zeros((b, RET_HEADS, RET_KEY_DIM, RET_VALUE_DIM), jnp.float32)
    _, prev = lax.scan(step, init, jnp.moveaxis(kv, 1, 0))
    prev = jnp.moveaxis(prev, 0, 1)
    cross = jnp.einsum('bnqhd,bnhde->bnqhe', qc, prev) * decay_in[None, None, :, :, None]
    return (intra + cross).reshape(b, l, RET_HEADS, RET_VALUE_DIM).astype(out_dtype)


def moe_ffn(u, w_router, b_router, w_gate_up, b_gate_up, w_down, b_down):
    shape = u.shape
    t = u.reshape(-1, D_MODEL)
    n = t.shape[0]
    logits = (t @ w_router + b_router).astype(jnp.float32)
    top_logit, top_idx = lax.top_k(logits, TOP_K)
    top_w = jax.nn.softmax(top_logit, axis=-1)
    e_flat = top_idx.reshape(-1)
    tok_flat = jnp.repeat(jnp.arange(n, dtype=jnp.int32), TOP_K)
    w_flat = top_w.reshape(-1)
    order = jnp.argsort(e_flat)
    e_sorted = e_flat[order]
    counts = jnp.bincount(e_flat, length=N_EXPERTS)
    padded = (counts + MOE_BLOCK - 1) // MOE_BLOCK * MOE_BLOCK
    start = jnp.cumsum(counts) - counts
    pend = jnp.cumsum(padded)
    pstart = pend - padded
    dest = pstart[e_sorted] + jnp.arange(n * TOP_K) - start[e_sorted]
    n_rows = (-(-(n * TOP_K) // MOE_BLOCK) + N_EXPERTS) * MOE_BLOCK
    n_blocks = n_rows // MOE_BLOCK
    row_tok = jnp.full((n_rows,), n, jnp.int32).at[dest].set(tok_flat[order])
    row_w = jnp.zeros((n_rows,), jnp.float32).at[dest].set(w_flat[order])
    block_expert = jnp.minimum(
        jnp.searchsorted(pend, jnp.arange(n_blocks) * MOE_BLOCK, side='right'), N_EXPERTS - 1)
    t_pad = jnp.concatenate([t, jnp.zeros((1, D_MODEL), t.dtype)], axis=0)

    def expert_block(args):
        rows, e = args
        xb = t_pad[rows]
        gu = xb @ w_gate_up[e] + b_gate_up[e]
        gate, up = jnp.split(gu, 2, axis=-1)
        gate = jnp.minimum(gate, SWIGLU_LIMIT)
        up = jnp.clip(up, -SWIGLU_LIMIT, SWIGLU_LIMIT)
        act = (up + 1.0) * gate * jax.nn.sigmoid(SWIGLU_ALPHA * gate)
        return act @ w_down[e] + b_down[e]

    out = lax.map(expert_block, (row_tok.reshape(n_blocks, MOE_BLOCK), block_expert))
    out = out.reshape(n_rows, D_MODEL) * row_w[:, None].astype(out.dtype)
    y = jnp.zeros((n + 1, D_MODEL), out.dtype).at[row_tok].add(out)[:n]
    return y.reshape(shape)


def setup_inputs(seed: int = 0) -> dict:
    key = jax.random.key(seed)
    ks = jax.random.split(key, 20)
    f32 = jnp.float32
    nrm = lambda k, s, sc: jax.random.normal(k, s, f32) * sc
    return {
        "x": nrm(ks[0], (BATCH, SEQ, D_MODEL), 1.0),
        "meta_tokens": nrm(ks[1], (N_META, D_MODEL), 1.0),
        "mix_norm_gain": 1.0 + nrm(ks[2], (DEPTH, D_MODEL), 0.02),
        "w_in": nrm(ks[3], (DEPTH, D_MODEL, IN_WIDTH), D_MODEL ** -0.5),
        "q_norm_gain": 1.0 + nrm(ks[4], (DEPTH, ATT_HEAD_DIM), 0.02),
        "k_norm_gain": 1.0 + nrm(ks[5], (DEPTH, ATT_HEAD_DIM), 0.02),
        "attn_sinks": nrm(ks[6], (DEPTH, ATT_HEADS), 0.5),
        "w_attn_branch": nrm(ks[7], (DEPTH, ATT_Q_W, D_MODEL), ATT_Q_W ** -0.5),
        "w_ret_branch": nrm(ks[8], (DEPTH, RET_V_W, D_MODEL), RET_V_W ** -0.5),
        "w_out": nrm(ks[9], (DEPTH, D_MODEL, D_MODEL), D_MODEL ** -0.5),
        "ffn_norm_gain": 1.0 + nrm(ks[10], (DEPTH, D_MODEL), 0.02),
        "w_router": nrm(ks[11], (DEPTH, D_MODEL, N_EXPERTS), D_MODEL ** -0.5),
        "b_router": nrm(ks[12], (DEPTH, N_EXPERTS), 0.01),
        "w_gate_up": nrm(ks[13], (DEPTH, N_EXPERTS, D_MODEL, 2 * D_FF), D_MODEL ** -0.5),
        "b_gate_up": nrm(ks[14], (DEPTH, N_EXPERTS, 2 * D_FF), 0.01),
        "w_down": nrm(ks[15], (DEPTH, N_EXPERTS, D_FF, D_MODEL), D_FF ** -0.5),
        "b_down": nrm(ks[16], (DEPTH, N_EXPERTS, D_MODEL), 0.01),
    }


def reference(x, meta_tokens, mix_norm_gain, w_in, q_norm_gain, k_norm_gain, attn_sinks,
              w_attn_branch, w_ret_branch, w_out, ffn_norm_gain, w_router, b_router,
              w_gate_up, b_gate_up, w_down, b_down):
    b = x.shape[0]
    h = jnp.concatenate([
        jnp.zeros((b, N_PAD, D_MODEL), x.dtype),
        jnp.broadcast_to(meta_tokens.astype(x.dtype)[None], (b, N_META, D_MODEL)),
        x], axis=1)
    l = h.shape[1]
    occupied = (jnp.arange(l) >= N_PAD)[None, :, None, None]
    for layer in range(DEPTH):
        u = rms_norm(h, mix_norm_gain[layer])
        proj = u @ w_in[layer]
        aq, ak, av, rq, rk, rv, rg, gates = jnp.split(proj, SPLIT_POINTS, axis=-1)
        aq = rms_norm(aq.reshape(b, l, ATT_HEADS, ATT_HEAD_DIM), q_norm_gain[layer])
        ak = rms_norm(ak.reshape(b, l, ATT_KV_HEADS, ATT_HEAD_DIM), k_norm_gain[layer])
        av = av.reshape(b, l, ATT_KV_HEADS, ATT_HEAD_DIM)
        y_att = sliding_window_attention(aq, ak, av, attn_sinks[layer])
        rq = rq.reshape(b, l, RET_HEADS, RET_KEY_DIM)
        rk = jnp.where(occupied, rk.reshape(b, l, RET_HEADS, RET_KEY_DIM), 0.0).astype(rq.dtype) * (RET_KEY_DIM ** -0.5)
        rv = rv.reshape(b, l, RET_HEADS, RET_VALUE_DIM)
        y_ret = rms_norm(chunkwise_retention(rq, rk, rv)).reshape(b, l, RET_V_W) * jax.nn.silu(rg)
        g_att, g_ret = jnp.split(jax.nn.sigmoid(gates), 2, axis=-1)
        merged = g_att * (y_att @ w_attn_branch[layer]) + g_ret * (y_ret @ w_ret_branch[layer])
        h = h + merged @ w_out[layer]
        if layer == DEPTH - 1:
            h = h[:, PREFIX:]
        h = h + moe_ffn(rms_norm(h, ffn_norm_gain[layer]), w_router[layer], b_router[layer],
                        w_gate_up[layer], b_gate_up[layer], w_down[layer], b_down[layer])
    return h
```

```python
import functools
import math

import numpy as np
import jax
import jax.numpy as jnp
from jax import lax
from jax.experimental import pallas as pl
from jax.experimental.pallas import tpu as pltpu

D_MODEL = 1024
N_META = 16
BLOCK = 128
N_PAD = BLOCK - N_META
ATT_HEADS = 16
ATT_KV_HEADS = 2
ATT_HEAD_DIM = 64
RET_HEADS = 4
RET_KEY_DIM = 128
RET_VALUE_DIM = 256
N_EXPERTS = 32
TOP_K = 4
D_FF = 1024
SWIGLU_ALPHA = 1.702
SWIGLU_LIMIT = 7.0
MOE_BLOCK = 256
EPS = 1e-5

Q0, Q1 = 0, 1024
KV0, KV1 = 1024, 1280
RQ0, RQ1 = 1280, 1792
RK0, RK1 = 1792, 2304
RV0, RV1 = 2304, 3328
RG0, RG1 = 3328, 4352
GT0, GT1 = 4352, 6400
IN_WIDTH = 6400

VMEM_LIMIT_BYTES = 56 * 1024 * 1024
NEG_BIG = -1e30

F32 = jnp.float32
BF16 = jnp.bfloat16


def _cparams(sem):
    return pltpu.CompilerParams(dimension_semantics=sem, vmem_limit_bytes=VMEM_LIMIT_BYTES)


def _const_spec(shape):
    nd = len(shape)
    return pl.BlockSpec(shape, lambda *_: (0,) * nd)


def _inproj_kernel(x_ref, gain_ref, w_ref, gq_ref, gkv_ref, grp_ref,
                   q_ref, k_ref, v_ref, rq_ref, rk_ref, rv_ref, rg_ref, gt_ref, *, n_masked):
    x = x_ref[...]
    ms = jnp.mean(x * x, axis=-1, keepdims=True)
    u = (x * lax.rsqrt(ms + EPS) * gain_ref[...]).astype(BF16)

    def proj(a, b):
        return jnp.dot(u, w_ref[:, a:b], preferred_element_type=F32)

    def head_norm(blk, gain):
        ss = jnp.dot((blk * blk).astype(BF16), grp_ref[...], preferred_element_type=F32)
        return blk * lax.rsqrt(ss * (1.0 / ATT_HEAD_DIM) + EPS) * gain

    scale = ATT_HEAD_DIM ** -0.5
    for c in range(4):
        blk = proj(Q0 + 256 * c, Q0 + 256 * (c + 1))
        q_ref[:, 256 * c:256 * (c + 1)] = (head_norm(blk, gq_ref[...]) * scale).astype(BF16)
    kv = proj(KV0, KV1)
    kvn = head_norm(kv, gkv_ref[...])
    k_ref[...] = kvn[:, :128].astype(BF16)
    v_ref[...] = kv[:, 128:].astype(BF16)
    rq_ref[...] = proj(RQ0, RQ1).astype(BF16)
    rk = proj(RK0, RK1) * (RET_KEY_DIM ** -0.5)
    if n_masked:
        row = lax.broadcasted_iota(jnp.int32, rk.shape, 0)
        rk = jnp.where(row >= n_masked, rk, 0.0)
    rk_ref[...] = rk.astype(BF16)
    for c in range(2):
        rv_ref[:, 512 * c:512 * (c + 1)] = proj(RV0 + 512 * c, RV0 + 512 * (c + 1)).astype(BF16)
        rg_ref[:, 512 * c:512 * (c + 1)] = proj(RG0 + 512 * c, RG0 + 512 * (c + 1)).astype(BF16)
    for c in range(4):
        gt_ref[:, 512 * c:512 * (c + 1)] = proj(GT0 + 512 * c, GT0 + 512 * (c + 1)).astype(BF16)


def _inproj(x2d, gain, w_bf, gq, gkv, grp, *, tm, n_masked):
    rows = x2d.shape[0]
    widths = (1024, 128, 128, 512, 512, 1024, 1024, 2048)
    row_spec = lambda w: pl.BlockSpec((tm, w), lambda i: (i, 0))
    return pl.pallas_call(
        functools.partial(_inproj_kernel, n_masked=n_masked),
        grid=(rows // tm,),
        in_specs=[row_spec(D_MODEL), _const_spec((1, D_MODEL)),
                  pl.BlockSpec((D_MODEL, IN_WIDTH), lambda i: (0, 0), pipeline_mode=pl.Buffered(1)),
                  _const_spec((1, 256)), _const_spec((1, 256)), _const_spec((256, 256))],
        out_specs=[row_spec(w) for w in widths],
        out_shape=[jax.ShapeDtypeStruct((rows, w), BF16) for w in widths],
        compiler_params=_cparams(("parallel",)),
    )(x2d, gain, w_bf, gq, gkv, grp)


def _attn_kernel(q_ref, kc_ref, vc_ref, kp_ref, vp_ref, km_ref, vm_ref, slope_ref, sink_ref, y_ref):
    i = pl.program_id(1)
    nkeys = 2 * BLOCK + N_META

    def lane_lt64(rows):
        return lax.broadcasted_iota(jnp.int32, (rows, 128), 1) < ATT_HEAD_DIM

    def stack_rows(ref_prev, ref_cur, ref_meta):
        return jnp.concatenate([ref_prev[...].astype(F32), ref_cur[...].astype(F32),
                                ref_meta[N_PAD:BLOCK, :].astype(F32)], axis=0)

    k_all = stack_rows(kp_ref, kc_ref, km_ref)
    v_all = stack_rows(vp_ref, vc_ref, vm_ref)
    k_rot = pltpu.roll(k_all, ATT_HEAD_DIM, axis=1)
    v_rot = pltpu.roll(v_all, ATT_HEAD_DIM, axis=1)
    lo = lane_lt64(nkeys)
    qlo = lane_lt64(BLOCK)

    ii = lax.broadcasted_iota(jnp.int32, (8 * BLOCK, nkeys), 0) & (BLOCK - 1)
    jj = lax.broadcasted_iota(jnp.int32, (8 * BLOCK, nkeys), 1)
    is_prev = jj < BLOCK
    is_cur = (jj >= BLOCK) & (jj < 2 * BLOCK)
    d_prev = BLOCK + ii - jj
    d_cur = ii - (jj - BLOCK)
    d_meta = (i * BLOCK + N_META) + ii - (jj - 2 * BLOCK)
    dist = jnp.where(is_prev, d_prev, jnp.where(is_cur, d_cur, d_meta)).astype(F32)
    prev_off = jnp.where(i >= 1, 0, BLOCK)
    ok = (is_prev & (jj > ii + prev_off)) | (is_cur & (d_cur >= 0)) | (jj >= 2 * BLOCK)

    for h in range(ATT_KV_HEADS):
        if h == 0:
            kk = jnp.where(lo, k_all, k_rot)
            v_lo = jnp.where(lo, v_all, 0.0)
            v_hi = jnp.where(lo, 0.0, v_rot)
        else:
            kk = jnp.where(lo, k_rot, k_all)
            v_lo = jnp.where(lo, v_rot, 0.0)
            v_hi = jnp.where(lo, 0.0, v_all)
        kk = kk.astype(BF16)
        v_lo = v_lo.astype(BF16)
        v_hi = v_hi.astype(BF16)
        pieces = []
        for p in range(4):
            q2 = q_ref[:, (4 * h + p) * 128:(4 * h + p + 1) * 128]
            zero = jnp.zeros_like(q2)
            pieces.append(jnp.where(qlo, q2, zero))
            pieces.append(jnp.where(qlo, zero, q2))
        qs = jnp.concatenate(pieces, axis=0)
        s = lax.dot_general(qs, kk, (((1,), (1,)), ((), ())), preferred_element_type=F32)
        s = jnp.where(ok, s - slope_ref[h] * dist, NEG_BIG)
        sink = sink_ref[h]
        m = jnp.maximum(jnp.max(s, axis=1, keepdims=True), sink)
        e = jnp.exp(s - m)
        den = jnp.sum(e, axis=1, keepdims=True) + jnp.exp(sink - m)
        inv = 1.0 / den
        e = e.astype(BF16)
        for p in range(4):
            r0 = 256 * p
            o_lo = jnp.dot(e[r0:r0 + 128], v_lo, preferred_element_type=F32) * inv[r0:r0 + 128]
            o_hi = jnp.dot(e[r0 + 128:r0 + 256], v_hi, preferred_element_type=F32) * inv[r0 + 128:r0 + 256]
            y_ref[:, (4 * h + p) * 128:(4 * h + p + 1) * 128] = (o_lo + o_hi).astype(BF16)


def _attention(q, k, v, k_meta, v_meta, slope_rows, sink_rows, *, batch, nb):
    rows = q.shape[0]
    cur = lambda w: pl.BlockSpec((BLOCK, w), lambda b, i: (b * nb + i, 0))
    prev = lambda w: pl.BlockSpec((BLOCK, w), lambda b, i: (b * nb + jnp.maximum(i - 1, 0), 0))
    return pl.pallas_call(
        _attn_kernel,
        grid=(batch, nb),
        in_specs=[cur(1024), cur(128), cur(128), prev(128), prev(128),
                  _const_spec((BLOCK, 128)), _const_spec((BLOCK, 128)),
                  _const_spec((ATT_KV_HEADS, 8 * BLOCK, 1)), _const_spec((ATT_KV_HEADS, 8 * BLOCK, 1))],
        out_specs=cur(1024),
        out_shape=jax.ShapeDtypeStruct((rows, 1024), BF16),
        compiler_params=_cparams(("parallel", "parallel")),
    )(q, k, v, k, v, k_meta, v_meta, slope_rows, sink_rows)


def _ret_kernel(rq_ref, rk_ref, rv_ref, rg_ref, rkm_ref, rvm_ref, dintra_ref, din_ref, dout_ref,
                y_ref, state_ref, *, decay_chunk):
    i = pl.program_id(1)
    tn = (((0,), (0,)), ((), ()))
    nt = (((1,), (1,)), ((), ()))

    def kv_outer(k_bf, v_bf, h):
        kd = (k_bf.astype(F32) * dout_ref[h]).astype(BF16)
        return lax.dot_general(kd, v_bf, tn, preferred_element_type=F32)

    @pl.when(i == 0)
    def _():
        for h in range(RET_HEADS):
            state_ref[h] = kv_outer(rkm_ref[:, 128 * h:128 * (h + 1)], rvm_ref[:, 256 * h:256 * (h + 1)], h)

    for h in range(RET_HEADS):
        q = rq_ref[:, 128 * h:128 * (h + 1)]
        k = rk_ref[:, 128 * h:128 * (h + 1)]
        v = rv_ref[:, 256 * h:256 * (h + 1)]
        st = state_ref[h]
        s = lax.dot_general(q, k, nt, preferred_element_type=F32) * dintra_ref[h]
        o = jnp.dot(s.astype(BF16), v, preferred_element_type=F32)
        o = o + jnp.dot(q, st.astype(BF16), preferred_element_type=F32) * din_ref[h]
        o = o * lax.rsqrt(jnp.mean(o * o, axis=-1, keepdims=True) + EPS)
        g = rg_ref[:, 256 * h:256 * (h + 1)].astype(F32)
        y_ref[:, 256 * h:256 * (h + 1)] = (o * (g * jax.nn.sigmoid(g))).astype(BF16)
        state_ref[h] = st * decay_chunk[h] + kv_outer(k, v, h)


def _retention(rq, rk, rv, rg, rk_meta, rv_meta, dintra, din, dout, decay_chunk, *, batch, nb):
    rows = rq.shape[0]
    cur = lambda w: pl.BlockSpec((BLOCK, w), lambda b, i: (b * nb + i, 0))
    return pl.pallas_call(
        functools.partial(_ret_kernel, decay_chunk=decay_chunk),
        grid=(batch, nb),
        in_specs=[cur(512), cur(512), cur(1024), cur(1024),
                  _const_spec((BLOCK, 512)), _const_spec((BLOCK, 1024)),
                  _const_spec((RET_HEADS, BLOCK, BLOCK)),
                  _const_spec((RET_HEADS, BLOCK, 1)), _const_spec((RET_HEADS, BLOCK, 1))],
        out_specs=cur(1024),
        out_shape=jax.ShapeDtypeStruct((rows, 1024), BF16),
        scratch_shapes=[pltpu.VMEM((RET_HEADS, RET_KEY_DIM, RET_VALUE_DIM), F32)],
        compiler_params=_cparams(("parallel", "arbitrary")),
    )(rq, rk, rv, rg, rk_meta, rv_meta, dintra, din, dout)


def _merge_kernel(x_ref, ya_ref, yr_ref, gt_ref, wa_ref, wr_ref, wo_ref, gain_ref,
                  wrt_hi_ref, wrt_lo_ref, brt_ref, h_ref, u_ref, lg_ref):
    a = jnp.dot(ya_ref[...], wa_ref[...], preferred_element_type=F32)
    r = jnp.dot(yr_ref[...], wr_ref[...], preferred_element_type=F32)
    g_att = jax.nn.sigmoid(gt_ref[:, :D_MODEL].astype(F32))
    g_ret = jax.nn.sigmoid(gt_ref[:, D_MODEL:].astype(F32))
    merged = (g_att * a + g_ret * r).astype(BF16)
    h = x_ref[...] + jnp.dot(merged, wo_ref[...], preferred_element_type=F32)
    h_ref[...] = h
    u = h * lax.rsqrt(jnp.mean(h * h, axis=-1, keepdims=True) + EPS) * gain_ref[...]
    u_ref[...] = u
    u_hi = u.astype(BF16)
    u_lo = (u - u_hi.astype(F32)).astype(BF16)
    lg = jnp.dot(u_hi, wrt_hi_ref[...], preferred_element_type=F32)
    lg = lg + jnp.dot(u_lo, wrt_hi_ref[...], preferred_element_type=F32)
    lg = lg + jnp.dot(u_hi, wrt_lo_ref[...], preferred_element_type=F32)
    lg_ref[...] = lg + brt_ref[...]


def _merge(x2d, y_att, y_ret, gates, wa, wr, wo, gain, wrt_hi, wrt_lo, brt, *, tm):
    rows = x2d.shape[0]
    row_spec = lambda w: pl.BlockSpec((tm, w), lambda i: (i, 0))
    return pl.pallas_call(
        _merge_kernel,
        grid=(rows // tm,),
        in_specs=[row_spec(1024), row_spec(1024), row_spec(1024), row_spec(2048),
                  _const_spec((1024, 1024)), _const_spec((1024, 1024)), _const_spec((1024, 1024)),
                  _const_spec((1, 1024)), _const_spec((1024, N_EXPERTS)), _const_spec((1024, N_EXPERTS)),
                  _const_spec((1, N_EXPERTS))],
        out_specs=[row_spec(1024), row_spec(1024), row_spec(N_EXPERTS)],
        out_shape=[jax.ShapeDtypeStruct((rows, 1024), F32), jax.ShapeDtypeStruct((rows, 1024), F32),
                   jax.ShapeDtypeStruct((rows, N_EXPERTS), F32)],
        compiler_params=_cparams(("parallel",)),
    )(x2d, y_att, y_ret, gates, wa, wr, wo, gain, wrt_hi, wrt_lo, brt)


def _dispatch_kernel(idx_ref, src_ref, dst_ref, sem):
    i = pl.program_id(0)
    n = pl.num_programs(0)
    rows = idx_ref.shape[-1]
    slot = i % 2

    def drain(step_slot):
        pltpu.make_async_copy(dst_ref.at[pl.ds(0, rows), :], dst_ref.at[pl.ds(0, rows), :],
                              sem.at[step_slot]).wait()

    def issue(r, carry):
        tok = idx_ref[0, 0, r]
        pltpu.make_async_copy(src_ref.at[pl.ds(tok, 1), :], dst_ref.at[pl.ds(i * rows + r, 1), :],
                              sem.at[slot]).start()
        return carry

    lax.fori_loop(0, rows, issue, 0)

    @pl.when(i > 0)
    def _():
        drain(1 - slot)

    @pl.when(i == n - 1)
    def _():
        drain(slot)


def _dispatch(row_tok, src, *, rows_per_step):
    n_rows = row_tok.shape[0]
    steps = n_rows // rows_per_step
    idx = row_tok.reshape(steps, 1, rows_per_step)
    return pl.pallas_call(
        _dispatch_kernel,
        grid=(steps,),
        in_specs=[pl.BlockSpec((1, 1, rows_per_step), lambda i: (i, 0, 0), memory_space=pltpu.SMEM),
                  pl.BlockSpec(memory_space=pl.ANY)],
        out_specs=pl.BlockSpec(memory_space=pl.ANY),
        out_shape=jax.ShapeDtypeStruct((n_rows, src.shape[1]), src.dtype),
        scratch_shapes=[pltpu.SemaphoreType.DMA((2,))],
        compiler_params=_cparams(("arbitrary",)),
    )(idx, src)


def _combine_kernel(idx_ref, idx_next_ref, h_ref, src_ref, y_ref, buf_ref, sem):
    i = pl.program_id(0)
    n = pl.num_programs(0)
    tt = h_ref.shape[0]
    slot = i % 2

    def issue_tile(ref, dst_slot):
        def body(j, carry):
            row = ref[0, 0, j]
            t = j // TOP_K
            k = j % TOP_K
            pltpu.make_async_copy(src_ref.at[pl.ds(row, 1), :], buf_ref.at[dst_slot, k, pl.ds(t, 1), :],
                                  sem.at[dst_slot]).start()
            return carry
        lax.fori_loop(0, tt * TOP_K, body, 0)

    @pl.when(i == 0)
    def _():
        issue_tile(idx_ref, slot)

    @pl.when(i + 1 < n)
    def _():
        issue_tile(idx_next_ref, 1 - slot)

    pltpu.make_async_copy(buf_ref.at[slot], buf_ref.at[slot], sem.at[slot]).wait()
    acc = h_ref[...]
    for k in range(TOP_K):
        acc = acc + buf_ref[slot, k]
    y_ref[...] = acc


def _combine(pos, h1, out_sorted, *, tt):
    n = h1.shape[0]
    steps = n // tt
    idx = pos.reshape(steps, 1, tt * TOP_K)
    return pl.pallas_call(
        _combine_kernel,
        grid=(steps,),
        in_specs=[pl.BlockSpec((1, 1, tt * TOP_K), lambda i: (i, 0, 0), memory_space=pltpu.SMEM),
                  pl.BlockSpec((1, 1, tt * TOP_K), lambda i: (jnp.minimum(i + 1, steps - 1), 0, 0),
                               memory_space=pltpu.SMEM),
                  pl.BlockSpec((tt, D_MODEL), lambda i: (i, 0)),
                  pl.BlockSpec(memory_space=pl.ANY)],
        out_specs=pl.BlockSpec((tt, D_MODEL), lambda i: (i, 0)),
        out_shape=jax.ShapeDtypeStruct((n, D_MODEL), F32),
        scratch_shapes=[pltpu.VMEM((2, TOP_K, tt, D_MODEL), F32), pltpu.SemaphoreType.DMA((2,))],
        compiler_params=_cparams(("arbitrary",)),
    )(idx, idx, h1, out_sorted)


def _ffn_kernel(be_ref, nused_ref, x_ref, wgu_ref, bgu_ref, wd_ref, bd_ref, rw_ref, o_ref,
                wgu_bf, wd_bf):
    i = pl.program_id(0)
    new_expert = (i == 0) | (be_ref[i] != be_ref[jnp.maximum(i - 1, 0)])

    @pl.when(new_expert)
    def _():
        wgu_bf[...] = wgu_ref[0].astype(BF16)
        wd_bf[...] = wd_ref[0].astype(BF16)

    @pl.when(i < nused_ref[0])
    def _():
        x = x_ref[...].astype(BF16)
        gu = jnp.dot(x, wgu_bf[...], preferred_element_type=F32) + bgu_ref[0]
        gate = jnp.minimum(gu[:, :D_FF], SWIGLU_LIMIT)
        up = jnp.clip(gu[:, D_FF:], -SWIGLU_LIMIT, SWIGLU_LIMIT)
        act = (up + 1.0) * gate * jax.nn.sigmoid(SWIGLU_ALPHA * gate)
        out = jnp.dot(act.astype(BF16), wd_bf[...], preferred_element_type=F32) + bd_ref[0]
        o_ref[...] = out * rw_ref[...]

    @pl.when(i >= nused_ref[0])
    def _():
        o_ref[...] = jnp.zeros_like(o_ref)


def _expert_ffn(block_expert, n_used, xs, w_gate_up, b_gate_up, w_down, b_down, row_w):
    n_rows = xs.shape[0]
    n_blocks = n_rows // MOE_BLOCK
    grid_spec = pltpu.PrefetchScalarGridSpec(
        num_scalar_prefetch=2,
        grid=(n_blocks,),
        in_specs=[pl.BlockSpec((MOE_BLOCK, D_MODEL), lambda i, be, nu: (i, 0)),
                  pl.BlockSpec((1, D_MODEL, 2 * D_FF), lambda i, be, nu: (be[i], 0, 0)),
                  pl.BlockSpec((1, 1, 2 * D_FF), lambda i, be, nu: (be[i], 0, 0)),
                  pl.BlockSpec((1, D_FF, D_MODEL), lambda i, be, nu: (be[i], 0, 0)),
                  pl.BlockSpec((1, 1, D_MODEL), lambda i, be, nu: (be[i], 0, 0)),
                  pl.BlockSpec((MOE_BLOCK, 1), lambda i, be, nu: (i, 0))],
        out_specs=pl.BlockSpec((MOE_BLOCK, D_MODEL), lambda i, be, nu: (i, 0)),
        scratch_shapes=[pltpu.VMEM((D_MODEL, 2 * D_FF), BF16), pltpu.VMEM((D_FF, D_MODEL), BF16)],
    )
    return pl.pallas_call(
        _ffn_kernel,
        grid_spec=grid_spec,
        out_shape=jax.ShapeDtypeStruct((n_rows, D_MODEL), F32),
        compiler_params=_cparams(("arbitrary",)),
    )(block_expert, n_used, xs, w_gate_up, b_gate_up.reshape(N_EXPERTS, 1, 2 * D_FF),
      w_down, b_down.reshape(N_EXPERTS, 1, D_MODEL), row_w.reshape(n_rows, 1))


def _routing(logits, n_rows):
    n = logits.shape[0]
    n_blocks = n_rows // MOE_BLOCK
    top_logit, top_idx = lax.top_k(logits, TOP_K)
    top_w = jax.nn.softmax(top_logit, axis=-1)
    e_flat = top_idx.reshape(-1).astype(jnp.int32)
    onehot = (e_flat[:, None] == jnp.arange(N_EXPERTS, dtype=jnp.int32)[None, :]).astype(jnp.int32)
    csum = jnp.cumsum(onehot, axis=0)
    rank = jnp.sum(csum * onehot, axis=1) - 1
    counts = csum[-1]
    padded = (counts + MOE_BLOCK - 1) // MOE_BLOCK * MOE_BLOCK
    pend = jnp.cumsum(padded)
    pstart = pend - padded
    pos = (pstart[e_flat] + rank).astype(jnp.int32)
    tok_flat = jnp.repeat(jnp.arange(n, dtype=jnp.int32), TOP_K)
    row_tok = jnp.zeros((n_rows,), jnp.int32).at[pos].set(tok_flat)
    row_w = jnp.zeros((n_rows,), F32).at[pos].set(top_w.reshape(-1))
    block_expert = jnp.minimum(
        jnp.searchsorted(pend, jnp.arange(n_blocks, dtype=jnp.int32) * MOE_BLOCK, side='right'),
        N_EXPERTS - 1).astype(jnp.int32)
    n_used = (pend[-1:] // MOE_BLOCK).astype(jnp.int32)
    return pos, row_tok, row_w, block_expert, n_used


def _retention_decays():
    log_g = np.log1p(-(2.0 ** (-5.0 - np.arange(RET_HEADS, dtype=np.float64))))
    idx = np.arange(BLOCK, dtype=np.float64)
    diff = idx[:, None] - idx[None, :]
    dintra = np.where(diff >= 0, np.exp(log_g[:, None, None] * np.maximum(diff, 0.0)), 0.0)
    din = np.exp(log_g[:, None] * (idx[None, :] + 1.0))[:, :, None]
    dout = np.exp(log_g[:, None] * (BLOCK - 1.0 - idx[None, :]))[:, :, None]
    dchunk = tuple(float(v) for v in np.exp(log_g * BLOCK))
    return (jnp.asarray(dintra, F32), jnp.asarray(din, F32), jnp.asarray(dout, F32), dchunk)


def kernel(x, meta_tokens, mix_norm_gain, w_in, q_norm_gain, k_norm_gain, attn_sinks, w_attn_branch,
           w_ret_branch, w_out, ffn_norm_gain, w_router, b_router, w_gate_up, b_gate_up, w_down, b_down):
    batch, seq, _ = x.shape
    assert seq % BLOCK == 0 and mix_norm_gain.shape[0] == 1
    nb = seq // BLOCK
    n = batch * seq
    tm = 512 if n % 512 == 0 else BLOCK
    x2d = x.reshape(n, D_MODEL)

    w_in_bf = w_in[0].astype(BF16)
    gain_mix = mix_norm_gain[0].reshape(1, D_MODEL)
    gq = jnp.tile(q_norm_gain[0], 4).reshape(1, 256)
    gkv = jnp.concatenate([jnp.tile(k_norm_gain[0], 2), jnp.ones((128,), F32)]).reshape(1, 256)
    grp = jnp.asarray(np.kron(np.eye(4), np.ones((ATT_HEAD_DIM, ATT_HEAD_DIM))), BF16)
    slopes = 2.0 ** (-8.0 * jnp.arange(1, ATT_HEADS + 1, dtype=F32) / ATT_HEADS)
    per_row = lambda a: jnp.repeat(a.astype(F32).reshape(ATT_KV_HEADS, 8), BLOCK, axis=1)[:, :, None]
    slope_rows = per_row(slopes)
    sink_rows = per_row(attn_sinks[0])
    dintra, din, dout, dchunk = _retention_decays()

    prefix = jnp.concatenate([jnp.zeros((N_PAD, D_MODEL), x.dtype), meta_tokens.astype(x.dtype)], axis=0)
    pm = _inproj(prefix, gain_mix, w_in_bf, gq, gkv, grp, tm=BLOCK, n_masked=N_PAD)
    q, k, v, rq, rk, rv, rg, gates = _inproj(x2d, gain_mix, w_in_bf, gq, gkv, grp, tm=tm, n_masked=0)
    y_att = _attention(q, k, v, pm[1], pm[2], slope_rows, sink_rows, batch=batch, nb=nb)
    y_ret = _retention(rq, rk, rv, rg, pm[4], pm[5], dintra, din, dout, dchunk, batch=batch, nb=nb)

    wrt = w_router[0]
    wrt_hi = wrt.astype(BF16)
    wrt_lo = (wrt - wrt_hi.astype(F32)).astype(BF16)
    h1, u2, logits = _merge(x2d, y_att, y_ret, gates, w_attn_branch[0].astype(BF16),
                            w_ret_branch[0].astype(BF16), w_out[0].astype(BF16),
                            ffn_norm_gain[0].reshape(1, D_MODEL), wrt_hi, wrt_lo,
                            b_router[0].reshape(1, N_EXPERTS), tm=tm)

    n_rows = (-(-(n * TOP_K) // MOE_BLOCK) + N_EXPERTS) * MOE_BLOCK
    pos, row_tok, row_w, block_expert, n_used = _routing(logits, n_rows)
    xs = _dispatch(row_tok, u2, rows_per_step=MOE_BLOCK)
    out_sorted = _expert_ffn(block_expert, n_used, xs, w_gate_up[0], b_gate_up[0], w_down[0], b_down[0], row_w)
    y = _combine(pos, h1, out_sorted, tt=BLOCK)
    return y.reshape(batch, seq, D_MODEL)
```
